```python
import math
import jax, jax.numpy as jnp
from jax import lax
import numpy as np

D_MODEL = 4096
BATCH = 2
SEQ = 8192
DEPTH = 4

HEAD_DIM = 128
H_DSA = 6
H_SB = 6
H_FOX = 6
DIL_GROUPS = ((128, 1), (512, 4), (2048, 16))
H_DIL_GROUP = 2
N_DIL_GROUPS = 3
H_DIL = H_DIL_GROUP * N_DIL_GROUPS
IDX_HEADS = 16
IDX_DIM = 64
TOPK_MAX = 256
GATE_RANK = 256
N_BRANCH = 4
D_FF = 4 * D_MODEL
ROPE_THETA = 10000.0
BLOCK_Q = 128
RMS_EPS = 1e-6

W_DSA = H_DSA * HEAD_DIM
W_SB = H_SB * HEAD_DIM
W_FOX = H_FOX * HEAD_DIM
W_DIL = H_DIL * HEAD_DIM
W_DIL_OUT = H_DIL_GROUP * HEAD_DIM
IN_SPLITS = (3 * W_DSA, 3 * W_SB, 3 * W_FOX, 3 * W_DIL,
             IDX_HEADS * IDX_DIM, IDX_DIM, IDX_HEADS, H_FOX, GATE_RANK)
N_IN = 3 * W_DSA + 3 * W_SB + 3 * W_FOX + 3 * W_DIL + IDX_HEADS * IDX_DIM + IDX_DIM + IDX_HEADS + H_FOX + GATE_RANK

kernel_name = "hybrid_gated_four_mixer_block"


def rms_norm(x, g):
    xf = x.astype(jnp.float32)
    y = xf * lax.rsqrt(jnp.mean(xf * xf, axis=-1, keepdims=True) + RMS_EPS)
    return (y * g.astype(jnp.float32)).astype(x.dtype)


def rotary_tables(seq, dim):
    inv = ROPE_THETA ** (-jnp.arange(0, dim, 2, dtype=jnp.float32) / dim)
    ang = jnp.arange(seq, dtype=jnp.float32)[:, None] * inv[None, :]
    ang = jnp.concatenate([ang, ang], axis=-1)
    return jnp.cos(ang), jnp.sin(ang)


def apply_rotary(x, rope):
    cos, sin = rope
    half = x.shape[-1] // 2
    rot = jnp.concatenate([-x[..., half:], x[..., :half]], axis=-1)
    return x * cos[:, None, :].astype(x.dtype) + rot * sin[:, None, :].astype(x.dtype)


def split_cols(z, sizes):
    out, start = [], 0
    for s in sizes:
        out.append(z[..., start:start + s])
        start += s
    return out


def to_blocks(a):
    b, l = a.shape[:2]
    return jnp.moveaxis(a.reshape((b, l // BLOCK_Q, BLOCK_Q) + a.shape[2:]), 1, 0)


def from_blocks(a):
    a = jnp.moveaxis(a, 0, 1)
    return a.reshape((a.shape[0], a.shape[1] * a.shape[2]) + a.shape[3:])


def block_starts(l):
    return jnp.arange(l // BLOCK_Q, dtype=jnp.int32) * BLOCK_Q


def dsa_attention(q, k, v, qi, ki, wi):
    l = q.shape[1]
    n_sel = min(TOPK_MAX, l // 4)
    key_pos = jnp.arange(l)

    def block(args):
        q_b, qi_b, wi_b, start = args
        t = start + jnp.arange(BLOCK_Q)
        causal = key_pos[None, :] <= t[:, None]
        dots = jnp.einsum('bqhd,bkd->bqhk', qi_b, ki).astype(jnp.float32) * IDX_DIM ** -0.5
        score = jnp.einsum('bqh,bqhk->bqk', wi_b.astype(jnp.float32) * IDX_HEADS ** -0.5,
                           jax.nn.relu(dots))
        score = jnp.where(causal[None], score, -jnp.inf)
        _, idx = lax.top_k(score, n_sel)
        valid = idx <= t[None, :, None]
        k_sel = jax.vmap(lambda kk, ii: kk[ii])(k, idx)
        v_sel = jax.vmap(lambda vv, ii: vv[ii])(v, idx)
        logits = jnp.einsum('bqhd,bqshd->bqhs', q_b, k_sel).astype(jnp.float32) * HEAD_DIM ** -0.5
        logits = jnp.where(valid[:, :, None, :], logits, -jnp.inf)
        p = jax.nn.softmax(logits, axis=-1).astype(v.dtype)
        return jnp.einsum('bqhs,bqshd->bqhd', p, v_sel)

    out = lax.map(block, (to_blocks(q), to_blocks(qi), to_blocks(wi), block_starts(l)))
    return from_blocks(out)


def stick_breaking_attention(q, k, v):
    l = q.shape[1]
    key_pos = jnp.arange(l)

    def block(args):
        q_b, start = args
        t = start + jnp.arange(BLOCK_Q)
        strict = key_pos[None, :] < t[:, None]
        z = jnp.einsum('bqhd,bkhd->bhqk', q_b, k).astype(jnp.float32) * HEAD_DIM ** -0.5
        log_beta = jax.nn.log_sigmoid(z)
        log_keep = jnp.where(strict, jax.nn.log_sigmoid(-z), 0.0)
        later = lax.cumsum(log_keep, axis=3, reverse=True) - log_keep
        a = jnp.where(strict, jnp.exp(log_beta + later), 0.0)
        return jnp.einsum('bhqk,bkhd->bqhd', a.astype(v.dtype), v)

    out = lax.map(block, (to_blocks(q), block_starts(l)))
    return from_blocks(out)


def forgetting_attention(q, k, v, log_f):
    l = q.shape[1]
    key_pos = jnp.arange(l)
    cum = jnp.cumsum(log_f, axis=1)
    cum_keys = jnp.moveaxis(cum, 1, 2)[:, :, None, :]

    def block(args):
        q_b, cum_b, start = args
        t = start + jnp.arange(BLOCK_Q)
        causal = key_pos[None, :] <= t[:, None]
        logits = jnp.einsum('bqhd,bkhd->bhqk', q_b, k).astype(jnp.float32) * HEAD_DIM ** -0.5
        logits = logits + jnp.moveaxis(cum_b, 1, 2)[..., None] - cum_keys
        logits = jnp.where(causal, logits, -jnp.inf)
        p = jax.nn.softmax(logits, axis=-1).astype(v.dtype)
        return jnp.einsum('bhqk,bkhd->bqhd', p, v)

    out = lax.map(block, (to_blocks(q), to_blocks(cum), block_starts(l)))
    return from_blocks(out)


def dilated_attention(qs, ks, vs):
    l = qs[0].shape[1]

    def block(args):
        q_bs, start = args
        t = start + jnp.arange(BLOCK_Q)
        outs, lses = [], []
        for (window, dil), q_b, k, v in zip(DIL_GROUPS, q_bs, ks, vs):
            n_keys = window // dil + 1
            pos = t[:, None] - dil * jnp.arange(n_keys)[None, :]
            valid = pos >= 0
            pos = jnp.maximum(pos, 0)
            k_sel = k[:, pos]
            v_sel = v[:, pos]
            logits = jnp.einsum('bqhd,bqkhd->bqhk', q_b, k_sel).astype(jnp.float32) * HEAD_DIM ** -0.5
            logits = jnp.where(valid[None, :, None, :], logits, -jnp.inf)
            lse = jax.nn.logsumexp(logits, axis=-1)
            p = jnp.exp(logits - lse[..., None]).astype(v.dtype)
            outs.append(jnp.einsum('bqhk,bqkhd->bqhd', p, v_sel))
            lses.append(lse)
        alpha = jax.nn.softmax(jnp.stack(lses), axis=0)
        return jnp.einsum('gbqh,gbqhd->bqhd', alpha.astype(outs[0].dtype), jnp.stack(outs))

    out = lax.map(block, (tuple(to_blocks(q) for q in qs), block_starts(l)))
    return from_blocks(out)


def hybrid_mixer(h, w_in, b_forget, w_gate_up, b_gate, w_br_dsa, w_br_sb, w_br_fox, w_br_dil,
                 w_out, rope_head, rope_idx):
    b, l, _ = h.shape
    z = h @ w_in
    qkv_dsa, qkv_sb, qkv_fox, qkv_dil, q_idx, k_idx, w_idx, f_logit, g_lat = split_cols(z, IN_SPLITS)

    def heads(a, n):
        return a.reshape(b, l, n, HEAD_DIM)

    q_a, k_a, v_a = [heads(a, H_DSA) for a in jnp.split(qkv_dsa, 3, axis=-1)]
    qi = apply_rotary(q_idx.reshape(b, l, IDX_HEADS, IDX_DIM), rope_idx)
    ki = apply_rotary(k_idx[:, :, None, :], rope_idx)[:, :, 0, :]
    o_a = dsa_attention(apply_rotary(q_a, rope_head), apply_rotary(k_a, rope_head), v_a, qi, ki, w_idx)

    q_b, k_b, v_b = [heads(a, H_SB) for a in jnp.split(qkv_sb, 3, axis=-1)]
    o_b = stick_breaking_attention(q_b, k_b, v_b)

    q_c, k_c, v_c = [heads(a, H_FOX) for a in jnp.split(qkv_fox, 3, axis=-1)]
    log_f = jax.nn.log_sigmoid((f_logit + b_forget).astype(jnp.float32))
    o_c = forgetting_attention(q_c, k_c, v_c, log_f)

    qkv_d = qkv_dil.reshape(b, l, N_DIL_GROUPS, 3, H_DIL_GROUP, HEAD_DIM)
    qs = [apply_rotary(qkv_d[:, :, g, 0], rope_head) for g in range(N_DIL_GROUPS)]
    ks = [apply_rotary(qkv_d[:, :, g, 1], rope_head) for g in range(N_DIL_GROUPS)]
    vs = [qkv_d[:, :, g, 2] for g in range(N_DIL_GROUPS)]
    o_d = dilated_attention(qs, ks, vs)

    gates = jax.nn.sigmoid(g_lat @ w_gate_up + b_gate).reshape(b, l, N_BRANCH, D_MODEL)
    merged = (gates[:, :, 0] * (o_a.reshape(b, l, W_DSA) @ w_br_dsa)
              + gates[:, :, 1] * (o_b.reshape(b, l, W_SB) @ w_br_sb)
              + gates[:, :, 2] * (o_c.reshape(b, l, W_FOX) @ w_br_fox)
              + gates[:, :, 3] * (o_d.reshape(b, l, W_DIL_OUT) @ w_br_dil))
    return merged @ w_out


def setup_inputs(seed: int = 0) -> dict:
    key = jax.random.key(seed)
    ks = jax.random.split(key, 16)
    f32 = jnp.float32

    def nrm(k, shape, fan_in):
        return jax.random.normal(k, shape, f32) * (fan_in ** -0.5)

    def gain(k):
        return 1.0 + 0.02 * jax.random.normal(k, (DEPTH, D_MODEL), f32)

    return {
        "x": jax.random.normal(ks[0], (BATCH, SEQ, D_MODEL), f32),
        "norm_mix_pre": gain(ks[1]),
        "norm_mix_post": gain(ks[2]),
        "norm_mlp_pre": gain(ks[3]),
        "norm_mlp_post": gain(ks[4]),
        "w_in": nrm(ks[5], (DEPTH, D_MODEL, N_IN), D_MODEL),
        "b_forget": 3.0 + 0.5 * jax.random.normal(ks[6], (DEPTH, H_FOX), f32),
        "w_gate_up": nrm(ks[7], (DEPTH, GATE_RANK, N_BRANCH * D_MODEL), GATE_RANK),
        "b_gate": 0.02 * jax.random.normal(ks[8], (DEPTH, N_BRANCH * D_MODEL), f32),
        "w_br_dsa": nrm(ks[9], (DEPTH, W_DSA, D_MODEL), W_DSA),
        "w_br_sb": nrm(ks[10], (DEPTH, W_SB, D_MODEL), W_SB),
        "w_br_fox": nrm(ks[11], (DEPTH, W_FOX, D_MODEL), W_FOX),
        "w_br_dil": nrm(ks[12], (DEPTH, W_DIL_OUT, D_MODEL), W_DIL_OUT),
        "w_out": nrm(ks[13], (DEPTH, D_MODEL, D_MODEL), D_MODEL),
        "w_ff1": nrm(ks[14], (DEPTH, D_MODEL, D_FF), D_MODEL),
        "w_ff2": nrm(ks[15], (DEPTH, D_FF, D_MODEL), D_FF),
    }


def reference(x, norm_mix_pre, norm_mix_post, norm_mlp_pre, norm_mlp_post, w_in, b_forget,
              w_gate_up, b_gate, w_br_dsa, w_br_sb, w_br_fox, w_br_dil, w_out, w_ff1, w_ff2):
    seq = x.shape[1]
    rope_head = rotary_tables(seq, HEAD_DIM)
    rope_idx = rotary_tables(seq, IDX_DIM)
    for layer in range(DEPTH):
        h = rms_norm(x, norm_mix_pre[layer])
        m = hybrid_mixer(h, w_in[layer], b_forget[layer], w_gate_up[layer], b_gate[layer],
                         w_br_dsa[layer], w_br_sb[layer], w_br_fox[layer], w_br_dil[layer],
                         w_out[layer], rope_head, rope_idx)
        x = x + rms_norm(m, norm_mix_post[layer])
        h = rms_norm(x, norm_mlp_pre[layer])
        f = jnp.square(jax.nn.relu(h @ w_ff1[layer])) @ w_ff2[layer]
        x = x + rms_norm(f, norm_mlp_post[layer])
    return x
```

```python
import functools
import math

import jax
import jax.numpy as jnp
from jax import lax
from jax.experimental import pallas as pl
from jax.experimental.pallas import tpu as pltpu

F32 = jnp.float32
BF16 = jnp.bfloat16
I32 = jnp.int32

HEAD_DIM = 128
H_DSA = 6
H_SB = 6
H_FOX = 6
DIL_GROUPS = ((128, 1), (512, 4), (2048, 16))
H_DIL_GROUP = 2
N_DIL_GROUPS = 3
IDX_HEADS = 16
IDX_DIM = 64
TOPK_MAX = 256
GATE_RANK = 256
N_BRANCH = 4
ROPE_THETA = 10000.0
RMS_EPS = 1e-6

W_HEADS = H_DSA * HEAD_DIM
N_MAIN = 3 * W_HEADS * 4
N_MAIN_HEADS = N_MAIN // HEAD_DIM
HD_DSA_Q, HD_DSA_K, HD_DSA_V = 0, 6, 12
HD_SB_Q, HD_SB_K, HD_SB_V = 18, 24, 30
HD_FOX_Q, HD_FOX_K, HD_FOX_V = 36, 42, 48
HD_DIL = 54
TAIL_QI = 0
TAIL_GLAT = IDX_HEADS * IDX_DIM
TAIL_KI = TAIL_GLAT + GATE_RANK
TAIL_WI = TAIL_KI + 128
TAIL_F = TAIL_WI + 128
N_TAIL = TAIL_F + 128

LANES = 128
VMEM_LIMIT = 56 * 1024 * 1024
MASK_VALUE = -2e30
M_INIT = -1e30
INT_MIN = -(2 ** 31)


def _cparams(sem):
    return pltpu.CompilerParams(dimension_semantics=sem, vmem_limit_bytes=VMEM_LIMIT)


def _nt_dot(a, b):
    return lax.dot_general(a, b, (((1,), (1,)), ((), ())), preferred_element_type=F32)


def _dot(a, b):
    return jnp.dot(a, b, preferred_element_type=F32)


def _rmsnorm_kernel(x_ref, g_ref, o_ref):
    x = x_ref[...]
    y = x * lax.rsqrt(jnp.mean(x * x, axis=-1, keepdims=True) + RMS_EPS)
    o_ref[...] = (y * g_ref[...]).astype(o_ref.dtype)


def _rmsnorm_bf16(x, g, tm=256):
    t, d = x.shape
    tm = min(tm, t)
    return pl.pallas_call(
        _rmsnorm_kernel,
        out_shape=jax.ShapeDtypeStruct((t, d), BF16),
        grid=(t // tm,),
        in_specs=[pl.BlockSpec((tm, d), lambda i: (i, 0)), pl.BlockSpec((1, d), lambda i: (0, 0))],
        out_specs=pl.BlockSpec((tm, d), lambda i: (i, 0)),
        compiler_params=_cparams(("parallel",)),
        name="rmsnorm_bf16",
    )(x, g.reshape(1, d))


def _residual_norm_kernel(x_ref, m_ref, g_ref, o_ref):
    m = m_ref[...]
    y = m * lax.rsqrt(jnp.mean(m * m, axis=-1, keepdims=True) + RMS_EPS)
    o_ref[...] = x_ref[...] + y * g_ref[...]


def _residual_norm(x, m, g, tm=256):
    t, d = x.shape
    tm = min(tm, t)
    return pl.pallas_call(
        _residual_norm_kernel,
        out_shape=jax.ShapeDtypeStruct((t, d), F32),
        grid=(t // tm,),
        in_specs=[pl.BlockSpec((tm, d), lambda i: (i, 0)), pl.BlockSpec((tm, d), lambda i: (i, 0)),
                  pl.BlockSpec((1, d), lambda i: (0, 0))],
        out_specs=pl.BlockSpec((tm, d), lambda i: (i, 0)),
        compiler_params=_cparams(("parallel",)),
        name="residual_norm",
    )(x, m, g.reshape(1, d))


def _mm_kernel(a_ref, b_ref, o_ref, *scratch, nk, relu2):
    def finish(acc):
        if relu2:
            acc = jnp.square(jnp.maximum(acc, 0.0))
        o_ref[...] = acc.astype(o_ref.dtype)

    if nk == 1:
        finish(_dot(a_ref[...], b_ref[...]))
        return
    acc_ref, = scratch
    k = pl.program_id(2)

    @pl.when(k == 0)
    def _():
        acc_ref[...] = jnp.zeros_like(acc_ref)

    acc_ref[...] += _dot(a_ref[...], b_ref[...])

    @pl.when(k == nk - 1)
    def _():
        finish(acc_ref[...])


def _pick(total, pref):
    t = min(pref, total)
    while total % t:
        t //= 2
    return t


def _matmul(a, b, out_dtype, *, tm=1024, tn=1024, tk=2048, relu2=False, name="matmul"):
    m, k = a.shape
    _, n = b.shape
    tm, tn, tk = _pick(m, tm), _pick(n, tn), _pick(k, tk)
    nk = k // tk
    scratch = [] if nk == 1 else [pltpu.VMEM((tm, tn), F32)]
    return pl.pallas_call(
        functools.partial(_mm_kernel, nk=nk, relu2=relu2),
        out_shape=jax.ShapeDtypeStruct((m, n), out_dtype),
        grid=(m // tm, n // tn, nk),
        in_specs=[pl.BlockSpec((tm, tk), lambda i, j, kk: (i, kk)),
                  pl.BlockSpec((tk, tn), lambda i, j, kk: (kk, j))],
        out_specs=pl.BlockSpec((tm, tn), lambda i, j, kk: (i, j)),
        scratch_shapes=scratch,
        compiler_params=_cparams(("parallel", "parallel", "arbitrary")),
        name=name,
    )(a, b)


def _rot_heads_in_block(j):
    return jnp.where(j < 2, 6, jnp.where(j >= 9, 4, 0))


def _inproj_kernel(a_ref, b_ref, cos_ref, sin_ref, o_ref):
    j = pl.program_id(1)
    acc = _dot(a_ref[...], b_ref[...])
    n_rot = _rot_heads_in_block(j)
    cos = cos_ref[...]
    sin = sin_ref[...]
    for c in range(W_HEADS // HEAD_DIM):
        y = acc[:, c * HEAD_DIM:(c + 1) * HEAD_DIM]

        @pl.when(c < n_rot)
        def _():
            o_ref[c] = (y * cos + pltpu.roll(y, HEAD_DIM // 2, 1) * sin).astype(o_ref.dtype)

        @pl.when(c >= n_rot)
        def _():
            o_ref[c] = y.astype(o_ref.dtype)


def _in_projection(h, w_main, cos, sin, seq, tm=1024):
    t, d = h.shape
    tm = _pick(seq, tm)
    n_pos = seq // tm
    return pl.pallas_call(
        _inproj_kernel,
        out_shape=jax.ShapeDtypeStruct((N_MAIN_HEADS, t, HEAD_DIM), BF16),
        grid=(t // tm, N_MAIN // W_HEADS),
        in_specs=[pl.BlockSpec((tm, d), lambda i, j: (i, 0)),
                  pl.BlockSpec((d, W_HEADS), lambda i, j: (0, j)),
                  pl.BlockSpec((tm, HEAD_DIM), lambda i, j: (i % n_pos, 0)),
                  pl.BlockSpec((tm, HEAD_DIM), lambda i, j: (i % n_pos, 0))],
        out_specs=pl.BlockSpec((W_HEADS // HEAD_DIM, tm, HEAD_DIM), lambda i, j: (j, i, 0)),
        compiler_params=_cparams(("parallel", "parallel")),
        name="in_projection",
    )(h, w_main, cos, sin)


def _split3(x):
    hi = x.astype(BF16)
    r = x - hi.astype(F32)
    mid = r.astype(BF16)
    lo = (r - mid.astype(F32)).astype(BF16)
    return hi, mid, lo


def _prep_kernel(t_ref, cos_ref, sin_ref, bf_ref, qi_ref, ki_ref, wi_ref, gl_ref, fc_ref, carry_ref):
    i = pl.program_id(1)
    tm = t_ref.shape[0]
    cos = cos_ref[...]
    sin = sin_ref[...]
    lane = lax.broadcasted_iota(I32, (tm, LANES), 1)
    first_half = (lane % IDX_DIM) < (IDX_DIM // 2)

    def rot64(y):
        r = jnp.where(first_half, -pltpu.roll(y, LANES - IDX_DIM // 2, 1), pltpu.roll(y, IDX_DIM // 2, 1))
        return y * cos + r * sin

    for p in range(IDX_HEADS // 2):
        y = rot64(t_ref[:, TAIL_QI + p * LANES:TAIL_QI + (p + 1) * LANES]).astype(BF16)
        qi_ref[2 * p] = y[:, :IDX_DIM]
        qi_ref[2 * p + 1] = y[:, IDX_DIM:]
    ki_ref[...] = rot64(t_ref[:, TAIL_KI:TAIL_KI + LANES])[:, :IDX_DIM].astype(BF16)
    wi_ref[...] = t_ref[:, TAIL_WI:TAIL_WI + IDX_HEADS] * (IDX_HEADS ** -0.5 * IDX_DIM ** -0.5)
    gl_ref[...] = t_ref[:, TAIL_GLAT:TAIL_GLAT + GATE_RANK].astype(BF16)

    @pl.when(i == 0)
    def _():
        carry_ref[...] = jnp.zeros_like(carry_ref)

    zf = t_ref[:, TAIL_F:TAIL_F + LANES] + bf_ref[...]
    log_f = jnp.minimum(zf, 0.0) - jnp.log(1.0 + jnp.exp(-jnp.abs(zf)))
    r_i = lax.broadcasted_iota(I32, (tm, tm), 0)
    c_i = lax.broadcasted_iota(I32, (tm, tm), 1)
    tri = jnp.where(c_i <= r_i, 1.0, 0.0).astype(BF16)
    hi, mid, lo = _split3(log_f)
    cum = _dot(tri, hi) + _dot(tri, mid) + _dot(tri, lo) + carry_ref[...]
    fc_ref[...] = cum
    carry_ref[...] = cum[tm - 1:tm, :]


def _tail_prep(tail, cos_i, sin_i, b_forget_pad, batch, seq, tm=256):
    t = tail.shape[0]
    tm = _pick(seq, tm)
    n_pos = seq // tm
    row = lambda b, i: (b * n_pos + i, 0)
    return pl.pallas_call(
        _prep_kernel,
        out_shape=(jax.ShapeDtypeStruct((IDX_HEADS, t, IDX_DIM), BF16),
                   jax.ShapeDtypeStruct((t, IDX_DIM), BF16),
                   jax.ShapeDtypeStruct((t, IDX_HEADS), F32),
                   jax.ShapeDtypeStruct((t, GATE_RANK), BF16),
                   jax.ShapeDtypeStruct((t, LANES), F32)),
        grid=(batch, n_pos),
        in_specs=[pl.BlockSpec((tm, N_TAIL), row),
                  pl.BlockSpec((tm, LANES), lambda b, i: (i, 0)),
                  pl.BlockSpec((tm, LANES), lambda b, i: (i, 0)),
                  pl.BlockSpec((1, LANES), lambda b, i: (0, 0))],
        out_specs=(pl.BlockSpec((IDX_HEADS, tm, IDX_DIM), lambda b, i: (0, b * n_pos + i, 0)),
                   pl.BlockSpec((tm, IDX_DIM), row),
                   pl.BlockSpec((tm, IDX_HEADS), row),
                   pl.BlockSpec((tm, GATE_RANK), row),
                   pl.BlockSpec((tm, LANES), row)),
        scratch_shapes=[pltpu.VMEM((1, LANES), F32)],
        compiler_params=_cparams(("parallel", "arbitrary")),
        name="tail_prep",
    )(tail, cos_i, sin_i, b_forget_pad)


def _softmax_step(s, v, m_ref, l_ref, acc_ref, idx):
    m_old = m_ref[idx]
    m_new = jnp.maximum(m_old, jnp.max(s, axis=1, keepdims=True))
    alpha = jnp.exp(m_old - m_new)
    p = jnp.exp(s - m_new)
    l_ref[idx] = alpha * l_ref[idx] + jnp.sum(p, axis=1, keepdims=True)
    acc_ref[idx] = alpha * acc_ref[idx] + _dot(p.astype(BF16), v)
    m_ref[idx] = m_new


def _init_state(m_ref, l_ref, acc_ref):
    m_ref[...] = jnp.full(m_ref.shape, M_INIT, F32)
    l_ref[...] = jnp.zeros(l_ref.shape, F32)
    acc_ref[...] = jnp.zeros(acc_ref.shape, F32)


def _dsa_kernel(qi_ref, ki_ref, wi_ref, q_ref, k_ref, v_ref, o_ref,
                key_ref, wrep_ref, m_ref, l_ref, acc_ref, *, n_sel, tk):
    i = pl.program_id(1)
    tq = q_ref.shape[1]
    n_chunks = i + 1
    n_lane_blocks = tk // LANES

    for h in range(IDX_HEADS):
        wrep_ref[h] = jnp.broadcast_to(wi_ref[:, h:h + 1], (tq, LANES))

    row = lax.broadcasted_iota(I32, (tq, tk), 0)
    col = lax.broadcasted_iota(I32, (tq, tk), 1)

    def score_chunk(c, diagonal):
        kc = ki_ref[pl.ds(pl.multiple_of(c * tk, tk), tk), :]
        parts = [jnp.zeros((tq, LANES), F32) for _ in range(n_lane_blocks)]
        for h in range(IDX_HEADS):
            y = jnp.maximum(_nt_dot(qi_ref[h], kc), 0.0)
            w = wrep_ref[h]
            for b in range(n_lane_blocks):
                parts[b] = parts[b] + w * y[:, b * LANES:(b + 1) * LANES]
        s = jnp.concatenate(parts, axis=1)
        bits = pltpu.bitcast(s, I32)
        key = bits ^ ((bits >> 31) & 0x7FFFFFFF)
        if diagonal:
            key = jnp.where(col <= row, key, INT_MIN)
        key_ref[c] = key

    score_chunk(i, True)

    def score_body(c, _):
        score_chunk(c, False)
        return 0

    lax.fori_loop(0, i, score_body, 0)

    def bisect(it, ans_u):
        bit = jnp.left_shift(jnp.int32(1), 31 - it)
        cand_u = ans_u | bit
        cand = cand_u ^ INT_MIN

        def count_chunk(c, cnt):
            kk = key_ref[c]
            for b in range(n_lane_blocks):
                cnt = cnt + jnp.where(kk[:, b * LANES:(b + 1) * LANES] >= cand, 1.0, 0.0)
            return cnt

        cnt = lax.fori_loop(0, n_chunks, count_chunk, jnp.zeros((tq, LANES), F32))
        total = jnp.sum(cnt, axis=1, keepdims=True)
        return jnp.where(total >= n_sel, cand_u, ans_u)

    ans_u = lax.fori_loop(0, 32, bisect, jnp.zeros((tq, LANES), I32))
    thresh = jnp.maximum(ans_u ^ INT_MIN, INT_MIN + 1)
    thresh = jnp.concatenate([thresh] * n_lane_blocks, axis=1)

    _init_state(m_ref, l_ref, acc_ref)
    scale = HEAD_DIM ** -0.5

    def attend_chunk(c, _):
        off = pl.multiple_of(c * tk, tk)
        bias = jnp.where(key_ref[c] >= thresh, 0.0, MASK_VALUE)
        for h in range(H_DSA):
            s = _nt_dot(q_ref[h], k_ref[h, pl.ds(off, tk), :]) * scale + bias
            _softmax_step(s, v_ref[h, pl.ds(off, tk), :], m_ref, l_ref, acc_ref, h)
        return 0

    lax.fori_loop(0, n_chunks, attend_chunk, 0)
    for h in range(H_DSA):
        o_ref[:, h * HEAD_DIM:(h + 1) * HEAD_DIM] = (acc_ref[h] / l_ref[h]).astype(o_ref.dtype)


def _dsa_attention(zh, qi, ki, wi, batch, seq, tq=256):
    t = zh.shape[1]
    tq = _pick(seq, tq)
    nq = seq // tq
    n_sel = min(TOPK_MAX, seq // 4)
    resident = pl.Buffered(1)
    return pl.pallas_call(
        functools.partial(_dsa_kernel, n_sel=n_sel, tk=tq),
        out_shape=jax.ShapeDtypeStruct((t, W_HEADS), BF16),
        grid=(batch, nq),
        in_specs=[pl.BlockSpec((IDX_HEADS, tq, IDX_DIM), lambda b, i: (0, b * nq + i, 0)),
                  pl.BlockSpec((seq, IDX_DIM), lambda b, i: (b, 0), pipeline_mode=resident),
                  pl.BlockSpec((tq, IDX_HEADS), lambda b, i: (b * nq + i, 0)),
                  pl.BlockSpec((H_DSA, tq, HEAD_DIM), lambda b, i: (HD_DSA_Q // H_DSA, b * nq + i, 0)),
                  pl.BlockSpec((H_DSA, seq, HEAD_DIM), lambda b, i: (HD_DSA_K // H_DSA, b, 0),
                               pipeline_mode=resident),
                  pl.BlockSpec((H_DSA, seq, HEAD_DIM), lambda b, i: (HD_DSA_V // H_DSA, b, 0),
                               pipeline_mode=resident)],
        out_specs=pl.BlockSpec((tq, W_HEADS), lambda b, i: (b * nq + i, 0)),
        scratch_shapes=[pltpu.VMEM((nq, tq, tq), I32),
                        pltpu.VMEM((IDX_HEADS, tq, LANES), F32),
                        pltpu.VMEM((H_DSA, tq, 1), F32),
                        pltpu.VMEM((H_DSA, tq, 1), F32),
                        pltpu.VMEM((H_DSA, tq, HEAD_DIM), F32)],
        compiler_params=_cparams(("parallel", "arbitrary")),
        name="dsa_attention",
    )(qi, ki, wi, zh, zh, zh)


def _sb_kernel(q_ref, k_ref, v_ref, o_ref, run_ref, acc_ref, *, tk):
    i = pl.program_id(2)
    tq = q_ref.shape[1]
    scale = HEAD_DIM ** -0.5
    q = q_ref[0]
    row = lax.broadcasted_iota(I32, (tq, tk), 0)
    col = lax.broadcasted_iota(I32, (tq, tk), 1)
    strict = col < row
    after = jnp.where(row > col, 1.0, 0.0).astype(BF16)
    run_ref[...] = jnp.zeros_like(run_ref)
    acc_ref[...] = jnp.zeros_like(acc_ref)

    def chunk(c, diagonal):
        off = pl.multiple_of(c * tk, tk)
        z = _nt_dot(q, k_ref[0, pl.ds(off, tk), :]) * scale
        sp = jnp.log(1.0 + jnp.exp(-jnp.abs(z)))
        log_beta = jnp.minimum(z, 0.0) - sp
        log_keep = -jnp.maximum(z, 0.0) - sp
        if diagonal:
            log_keep = jnp.where(strict, log_keep, 0.0)
        hi = log_keep.astype(BF16)
        lo = (log_keep - hi.astype(F32)).astype(BF16)
        later = run_ref[...] + _dot(hi, after) + _dot(lo, after)
        a = jnp.exp(log_beta + later)
        if diagonal:
            a = jnp.where(strict, a, 0.0)
        acc_ref[...] += _dot(a.astype(BF16), v_ref[0, pl.ds(off, tk), :])
        run_ref[...] += jnp.sum(log_keep, axis=1, keepdims=True)

    chunk(i, True)

    def body(n, _):
        chunk(i - 1 - n, False)
        return 0

    lax.fori_loop(0, i, body, 0)
    o_ref[...] = acc_ref[...].astype(o_ref.dtype)


def _sb_attention(zh, batch, seq, tq=256):
    t = zh.shape[1]
    tq = _pick(seq, tq)
    nq = seq // tq
    return pl.pallas_call(
        functools.partial(_sb_kernel, tk=tq),
        out_shape=jax.ShapeDtypeStruct((t, W_HEADS), BF16),
        grid=(batch, H_SB, nq),
        in_specs=[pl.BlockSpec((1, tq, HEAD_DIM), lambda b, h, i: (HD_SB_Q + h, b * nq + i, 0)),
                  pl.BlockSpec((1, seq, HEAD_DIM), lambda b, h, i: (HD_SB_K + h, b, 0)),
                  pl.BlockSpec((1, seq, HEAD_DIM), lambda b, h, i: (HD_SB_V + h, b, 0))],
        out_specs=pl.BlockSpec((tq, HEAD_DIM), lambda b, h, i: (b * nq + i, h)),
        scratch_shapes=[pltpu.VMEM((tq, 1), F32), pltpu.VMEM((tq, HEAD_DIM), F32)],
        compiler_params=_cparams(("parallel", "parallel", "arbitrary")),
        name="sb_attention",
    )(zh, zh, zh)


def _fox_kernel(q_ref, k_ref, v_ref, fc_ref, fr_ref, o_ref, m_ref, l_ref, acc_ref, *, tk):
    h = pl.program_id(1)
    i = pl.program_id(2)
    tq = q_ref.shape[1]
    scale = HEAD_DIM ** -0.5
    q = q_ref[0]
    lane = lax.broadcasted_iota(I32, (tq, LANES), 1)
    f_q = jnp.sum(jnp.where(lane == h, fc_ref[...], 0.0), axis=1, keepdims=True)
    row = lax.broadcasted_iota(I32, (tq, tk), 0)
    col = lax.broadcasted_iota(I32, (tq, tk), 1)
    _init_state(m_ref, l_ref, acc_ref)

    def chunk(c, diagonal):
        off = pl.multiple_of(c * tk, tk)
        s = _nt_dot(q, k_ref[0, pl.ds(off, tk), :]) * scale + f_q - fr_ref[0, 0, c]
        if diagonal:
            s = jnp.where(col <= row, s, MASK_VALUE)
        _softmax_step(s, v_ref[0, pl.ds(off, tk), :], m_ref, l_ref, acc_ref, 0)

    chunk(i, True)

    def body(c, _):
        chunk(c, False)
        return 0

    lax.fori_loop(0, i, body, 0)
    o_ref[...] = (acc_ref[0] / l_ref[0]).astype(o_ref.dtype)


def _fox_attention(zh, fc, f_rows, batch, seq, tq=256):
    t = zh.shape[1]
    tq = _pick(seq, tq)
    nq = seq // tq
    f_rows = f_rows.reshape(batch, H_FOX, nq, 1, tq)
    return pl.pallas_call(
        functools.partial(_fox_kernel, tk=tq),
        out_shape=jax.ShapeDtypeStruct((t, W_HEADS), BF16),
        grid=(batch, H_FOX, nq),
        in_specs=[pl.BlockSpec((1, tq, HEAD_DIM), lambda b, h, i: (HD_FOX_Q + h, b * nq + i, 0)),
                  pl.BlockSpec((1, seq, HEAD_DIM), lambda b, h, i: (HD_FOX_K + h, b, 0)),
                  pl.BlockSpec((1, seq, HEAD_DIM), lambda b, h, i: (HD_FOX_V + h, b, 0)),
                  pl.BlockSpec((tq, LANES), lambda b, h, i: (b * nq + i, 0)),
                  pl.BlockSpec((1, 1, nq, 1, tq), lambda b, h, i: (b, h, 0, 0, 0))],
        out_specs=pl.BlockSpec((tq, HEAD_DIM), lambda b, h, i: (b * nq + i, h)),
        scratch_shapes=[pltpu.VMEM((1, tq, 1), F32), pltpu.VMEM((1, tq, 1), F32),
                        pltpu.VMEM((1, tq, HEAD_DIM), F32)],
        compiler_params=_cparams(("parallel", "parallel", "arbitrary")),
        name="fox_attention",
    )(zh, zh, zh, fc, f_rows)


def _dil_kernel(*refs, tk):
    q_refs, k_refs, v_refs = refs[0:3], refs[3:6], refs[6:9]
    o_ref, m_ref, l_ref, acc_ref = refs[9:13]
    i = pl.program_id(2)
    tq = o_ref.shape[0]
    scale = HEAD_DIM ** -0.5
    row = lax.broadcasted_iota(I32, (tq, tk), 0)
    col = lax.broadcasted_iota(I32, (tq, tk), 1)
    _init_state(m_ref, l_ref, acc_ref)

    for g, (window, dil) in enumerate(DIL_GROUPS):
        q = q_refs[g][0]
        reject_bits = ~(window - 1) | (dil - 1)
        for back in range(-(-window // tk) + 1):
            dist = row - col + back * tk

            def step(g=g, back=back, dist=dist, q=q, window=window, reject_bits=reject_bits):
                off = pl.multiple_of((i - back) * tk, tk)
                s = _nt_dot(q, k_refs[g][0, pl.ds(off, tk), :]) * scale
                keep = (jnp.where(dist == window, 0, dist & reject_bits)) == 0
                s = jnp.where(keep, s, MASK_VALUE)
                _softmax_step(s, v_refs[g][0, pl.ds(off, tk), :], m_ref, l_ref, acc_ref, 0)

            if back == 0:
                step()
            else:
                pl.when(i >= back)(step)
    o_ref[...] = (acc_ref[0] / l_ref[0]).astype(o_ref.dtype)


def _dil_attention(zh, batch, seq, tq=256):
    t = zh.shape[1]
    tq = _pick(seq, tq)
    nq = seq // tq

    def head(g, part):
        return HD_DIL + g * 3 * H_DIL_GROUP + part * H_DIL_GROUP

    q_specs = [pl.BlockSpec((1, tq, HEAD_DIM), lambda b, s, i, g=g: (head(g, 0) + s, b * nq + i, 0))
               for g in range(N_DIL_GROUPS)]
    k_specs = [pl.BlockSpec((1, seq, HEAD_DIM), lambda b, s, i, g=g: (head(g, 1) + s, b, 0))
               for g in range(N_DIL_GROUPS)]
    v_specs = [pl.BlockSpec((1, seq, HEAD_DIM), lambda b, s, i, g=g: (head(g, 2) + s, b, 0))
               for g in range(N_DIL_GROUPS)]
    return pl.pallas_call(
        functools.partial(_dil_kernel, tk=tq),
        out_shape=jax.ShapeDtypeStruct((t, H_DIL_GROUP * HEAD_DIM), BF16),
        grid=(batch, H_DIL_GROUP, nq),
        in_specs=q_specs + k_specs + v_specs,
        out_specs=pl.BlockSpec((tq, HEAD_DIM), lambda b, s, i: (b * nq + i, s)),
        scratch_shapes=[pltpu.VMEM((1, tq, 1), F32), pltpu.VMEM((1, tq, 1), F32),
                        pltpu.VMEM((1, tq, HEAD_DIM), F32)],
        compiler_params=_cparams(("parallel", "parallel", "arbitrary")),
        name="dil_attention",
    )(*([zh] * 9))


def _merge_kernel(gl_ref, oa_ref, ob_ref, oc_ref, od_ref, wg0, wg1, wg2, wg3, bg0, bg1, bg2, bg3,
                  wa_ref, wb_ref, wc_ref, wd_ref, o_ref):
    gl = gl_ref[...]
    total = None
    for o_br, w_br, wg, bg in ((oa_ref, wa_ref, wg0, bg0), (ob_ref, wb_ref, wg1, bg1),
                               (oc_ref, wc_ref, wg2, bg2), (od_ref, wd_ref, wg3, bg3)):
        gate = 1.0 / (1.0 + jnp.exp(-(_dot(gl, wg[...]) + bg[...])))
        term = gate * _dot(o_br[...], w_br[...])
        total = term if total is None else total + term
    o_ref[...] = total.astype(o_ref.dtype)


def _gated_merge(glat, o_a, o_b, o_c, o_d, w_gate, b_gate, w_a, w_b, w_c, w_d, tm=1024, tn=512):
    t = glat.shape[0]
    d = w_a.shape[1]
    tm, tn = _pick(t, tm), _pick(d, tn)
    nj = d // tn
    rows = lambda width: pl.BlockSpec((tm, width), lambda i, j: (i, 0))
    gate_w = [pl.BlockSpec((GATE_RANK, tn), lambda i, j, br=br: (0, br * nj + j)) for br in range(N_BRANCH)]
    gate_b = [pl.BlockSpec((1, tn), lambda i, j, br=br: (0, br * nj + j)) for br in range(N_BRANCH)]
    br_w = lambda width: pl.BlockSpec((width, tn), lambda i, j: (0, j))
    return pl.pallas_call(
        _merge_kernel,
        out_shape=jax.ShapeDtypeStruct((t, d), BF16),
        grid=(t // tm, nj),
        in_specs=[rows(GATE_RANK), rows(W_HEADS), rows(W_HEADS), rows(W_HEADS), rows(H_DIL_GROUP * HEAD_DIM)]
        + gate_w + gate_b + [br_w(W_HEADS), br_w(W_HEADS), br_w(W_HEADS), br_w(H_DIL_GROUP * HEAD_DIM)],
        out_specs=pl.BlockSpec((tm, tn), lambda i, j: (i, j)),
        compiler_params=_cparams(("parallel", "parallel")),
        name="gated_merge",
    )(glat, o_a, o_b, o_c, o_d, w_gate, w_gate, w_gate, w_gate, b_gate, b_gate, b_gate, b_gate,
      w_a, w_b, w_c, w_d)


def _rotary_tables(seq, dim):
    inv = ROPE_THETA ** (-jnp.arange(0, dim, 2, dtype=F32) / dim)
    ang = jnp.arange(seq, dtype=F32)[:, None] * inv[None, :]
    ang = jnp.concatenate([ang, ang], axis=-1)
    return jnp.cos(ang), jnp.sin(ang)


def _tail_weight(w_in_l):
    d = w_in_l.shape[0]
    c = N_MAIN
    q_idx = w_in_l[:, c:c + IDX_HEADS * IDX_DIM]
    c += IDX_HEADS * IDX_DIM
    k_idx = w_in_l[:, c:c + IDX_DIM]
    c += IDX_DIM
    w_idx = w_in_l[:, c:c + IDX_HEADS]
    c += IDX_HEADS
    f_log = w_in_l[:, c:c + H_FOX]
    c += H_FOX
    g_lat = w_in_l[:, c:c + GATE_RANK]
    pad = lambda a: jnp.pad(a, ((0, 0), (0, LANES - a.shape[1])))
    return jnp.concatenate([q_idx, g_lat, pad(k_idx), pad(w_idx), pad(f_log)], axis=1)


def kernel(x, norm_mix_pre, norm_mix_post, norm_mlp_pre, norm_mlp_post, w_in, b_forget, w_gate_up, b_gate,
           w_br_dsa, w_br_sb, w_br_fox, w_br_dil, w_out, w_ff1, w_ff2):
    batch, seq, d = x.shape
    depth = w_in.shape[0]
    t = batch * seq
    xf = x.reshape(t, d)

    cos_h, sin_h = _rotary_tables(seq, HEAD_DIM)
    sin_h = sin_h * jnp.where(jnp.arange(HEAD_DIM) < HEAD_DIM // 2, -1.0, 1.0)
    cos_i, sin_i = _rotary_tables(seq, IDX_DIM)
    cos_i = jnp.concatenate([cos_i, cos_i], axis=1)
    sin_i = jnp.concatenate([sin_i, sin_i], axis=1)

    for layer in range(depth):
        w_main = w_in[layer, :, :N_MAIN].astype(BF16)
        w_tail = _tail_weight(w_in[layer]).astype(BF16)
        bf_pad = jnp.pad(b_forget[layer], (0, LANES - H_FOX)).reshape(1, LANES)

        h = _rmsnorm_bf16(xf, norm_mix_pre[layer])
        zh = _in_projection(h, w_main, cos_h, sin_h, seq)
        tail = _matmul(h, w_tail, F32, tn=N_TAIL, name="tail_projection")
        qi, ki, wi, glat, fc = _tail_prep(tail, cos_i, sin_i, bf_pad, batch, seq)
        f_rows = jnp.transpose(fc.reshape(batch, seq, LANES)[:, :, :H_FOX], (0, 2, 1))

        o_a = _dsa_attention(zh, qi, ki, wi, batch, seq)
        o_b = _sb_attention(zh, batch, seq)
        o_c = _fox_attention(zh, fc, f_rows, batch, seq)
        o_d = _dil_attention(zh, batch, seq)

        merged = _gated_merge(glat, o_a, o_b, o_c, o_d, w_gate_up[layer].astype(BF16),
                              b_gate[layer].reshape(1, -1), w_br_dsa[layer].astype(BF16),
                              w_br_sb[layer].astype(BF16), w_br_fox[layer].astype(BF16),
                              w_br_dil[layer].astype(BF16))
        mix = _matmul(merged, w_out[layer].astype(BF16), F32, name="out_projection")
        xf = _residual_norm(xf, mix, norm_mix_post[layer])

        h2 = _rmsnorm_bf16(xf, norm_mlp_pre[layer])
        u = _matmul(h2, w_ff1[layer].astype(BF16), BF16, relu2=True, name="mlp_up")
        f = _matmul(u, w_ff2[layer].astype(BF16), F32, name="mlp_down")
        xf = _residual_norm(xf, f, norm_mlp_post[layer])

    return xf.reshape(batch, seq, d)
```
